```python
import math
import jax, jax.numpy as jnp
from jax import lax
import numpy as np

D_MODEL = 2048
BATCH = 1
SEQ = 8192
DEPTH = 2
DEC_BATCH = 16
DEC_SEQ = 2048
PAST_LEN = 128

HEAD_DIM = 128
A_HEADS = 6
A_KV_HEADS = 2
B_HEADS = 6
B_PAIRS = ((128, 1), (512, 4), (2048, 16))
B_HEADS_PER_PAIR = 2
C_HEADS = 4
C_DK = 64
C_DV = 128
C_GATE_RANK = 16
C_GATE_NORM = 16.0
C_CHUNK = 16
GRID_W = 64
ROPE_THETA = 10000.0
Q_BLOCK = 128
REL_BUCKETS = 32
REL_MAX_DIST = 1024
N_EXPERTS = 16
EC_CAPACITY = 2
D_FF_EXPERT = 2048
DN_ALPHA = (2 * DEPTH) ** 0.25
DN_BETA = (8 * DEPTH) ** -0.25
LN_EPS = 1e-5
RMS_EPS = 1e-6
NEG_INF = -1e30

A_W = A_HEADS * HEAD_DIM
A_KV_W = A_KV_HEADS * HEAD_DIM
B_W = B_HEADS * HEAD_DIM
C_K_W = C_HEADS * C_DK
C_V_W = C_HEADS * C_DV
D_MIX = A_W + B_W + C_V_W
IN_SIZES = (A_W, A_KV_W, A_KV_W, B_W, B_W, B_W, C_K_W, C_K_W, C_V_W, C_V_W, C_GATE_RANK, C_GATE_RANK)
D_IN = 5152

kernel_name = "hybrid_bidir_encoder_gqa_dilated_gla_ec"

f32 = jnp.float32


def layer_norm(x, g, b):
    xf = x.astype(f32)
    mu = jnp.mean(xf, -1, keepdims=True)
    var = jnp.mean(jnp.square(xf - mu), -1, keepdims=True)
    return ((xf - mu) * lax.rsqrt(var + LN_EPS) * g.astype(f32) + b.astype(f32)).astype(x.dtype)


def rms_norm(x, g):
    xf = x.astype(f32)
    return xf * lax.rsqrt(jnp.mean(xf * xf, -1, keepdims=True) + RMS_EPS) * g.astype(f32)


def split_columns(proj):
    outs, start = [], 0
    for size in IN_SIZES:
        outs.append(proj[..., start:start + size])
        start += size
    return outs


def axial_rope_tables(T):
    rows = T // GRID_W
    r = jnp.broadcast_to(jnp.arange(rows)[:, None], (rows, GRID_W)).reshape(T).astype(f32)
    c = jnp.broadcast_to(jnp.arange(GRID_W)[None, :], (rows, GRID_W)).reshape(T).astype(f32)
    axis_dim = HEAD_DIM // 2
    inv = ROPE_THETA ** (-jnp.arange(0, axis_dim, 2, dtype=f32) / axis_dim)
    ang_r = r[:, None] * inv[None, :]
    ang_c = c[:, None] * inv[None, :]
    return jnp.cos(ang_r), jnp.sin(ang_r), jnp.cos(ang_c), jnp.sin(ang_c)


def rope_half(x, cos, sin):
    h = x.shape[-1] // 2
    x1, x2 = x[..., :h], x[..., h:]
    c = cos[:, None, :]
    s = sin[:, None, :]
    return jnp.concatenate([x1 * c - x2 * s, x1 * s + x2 * c], -1)


def apply_axial_rope(x, tabs):
    cr, sr, cc, sc = tabs
    h = HEAD_DIM // 2
    return jnp.concatenate([rope_half(x[..., :h], cr, sr), rope_half(x[..., h:], cc, sc)], -1)


def mixer_a(q, k, v, qn_g, kn_g):
    Bt, T = q.shape[:2]
    dt = q.dtype
    tabs = axial_rope_tables(T)
    q = (apply_axial_rope(rms_norm(q, qn_g), tabs) * HEAD_DIM ** -0.5).astype(dt)
    k = apply_axial_rope(rms_norm(k, kn_g), tabs).astype(dt)
    grp = A_HEADS // A_KV_HEADS
    nb = T // Q_BLOCK
    qb = q.reshape(Bt, nb, Q_BLOCK, A_KV_HEADS, grp, HEAD_DIM).transpose(1, 0, 2, 3, 4, 5)

    def block(qi):
        s = jnp.einsum('bqkgd,bskd->bkgqs', qi, k, preferred_element_type=f32)
        p = jax.nn.softmax(s, axis=-1).astype(dt)
        return jnp.einsum('bkgqs,bskd->bqkgd', p, v)

    o = lax.map(block, qb)
    return o.transpose(1, 0, 2, 3, 4, 5).reshape(Bt, T, A_W)


def t5_bucket(rel):
    half = REL_BUCKETS // 2
    max_exact = half // 2
    n = jnp.abs(rel)
    large = max_exact + (jnp.log(jnp.maximum(n, 1).astype(f32) / max_exact)
                         / math.log(REL_MAX_DIST / max_exact) * (half - max_exact)).astype(jnp.int32)
    large = jnp.minimum(large, half - 1)
    return jnp.where(rel > 0, half, 0) + jnp.where(n < max_exact, n, large)


def dilated_group(q, k, v, bias_table, window, dil):
    Bt, T, h, hd = q.shape
    R = (window // 2) // dil
    L = T // dil
    nb = -(-L // R)
    Lp = nb * R
    N = Bt * dil

    def to_sub(x):
        x = x.reshape(Bt, L, dil, h, hd).transpose(0, 2, 1, 3, 4).reshape(N, L, h, hd)
        return jnp.pad(x, ((0, 0), (0, Lp - L), (0, 0), (0, 0)))

    def key_blocks(x):
        xb = jnp.pad(to_sub(x).reshape(N, nb, R, h, hd), ((0, 0), (1, 1), (0, 0), (0, 0), (0, 0)))
        return jnp.concatenate([xb[:, :-2], xb[:, 1:-1], xb[:, 2:]], axis=2)

    qs = to_sub(q).reshape(N, nb, R, h, hd)
    ks = key_blocks(k)
    vs = key_blocks(v)
    rel = jnp.arange(3 * R)[None, :] - R - jnp.arange(R)[:, None]
    u_k = jnp.arange(nb)[:, None] * R - R + jnp.arange(3 * R)[None, :]
    valid = (jnp.abs(rel) <= R)[None] & ((u_k >= 0) & (u_k < L))[:, None, :]
    bias = bias_table.astype(f32)[t5_bucket(rel * dil)].transpose(2, 0, 1)
    s = jnp.einsum('nbqhd,nbkhd->nbhqk', qs, ks, preferred_element_type=f32) * hd ** -0.5 + bias[None, None]
    s = jnp.where(valid[None, :, None], s, NEG_INF)
    m = jnp.max(s, -1, keepdims=True)
    p = jnp.exp(s - m)
    den = jnp.sum(p, -1, keepdims=True)
    o = jnp.einsum('nbhqk,nbkhd->nbqhd', (p / den).astype(q.dtype), vs)
    lse = (m + jnp.log(den))[..., 0]
    o = o.reshape(Bt, dil, Lp, h, hd)[:, :, :L].transpose(0, 2, 1, 3, 4).reshape(Bt, T, h, hd)
    lse = lse.transpose(0, 1, 3, 2).reshape(Bt, dil, Lp, h)[:, :, :L].transpose(0, 2, 1, 3).reshape(Bt, T, h)
    return o, lse


def mixer_b(q, k, v, rel_bias):
    Bt, T = q.shape[:2]
    outs, lses = [], []
    for g, (window, dil) in enumerate(B_PAIRS):
        hs = slice(g * B_HEADS_PER_PAIR, (g + 1) * B_HEADS_PER_PAIR)
        o, l = dilated_group(q[:, :, hs], k[:, :, hs], v[:, :, hs], rel_bias[:, hs], window, dil)
        outs.append(o)
        lses.append(l)
    o = jnp.stack(outs, 2)
    wts = jax.nn.softmax(jnp.stack(lses, 2), axis=2)
    return (o * wts[..., None].astype(o.dtype)).reshape(Bt, T, B_W)


def gla_direction(q, k, v, log_a):
    Bt, T, H, dk = q.shape
    dv = v.shape[-1]
    nc = T // C_CHUNK
    q, k, log_a = (x.reshape(Bt, nc, C_CHUNK, H, dk) for x in (q, k, log_a))
    v = v.reshape(Bt, nc, C_CHUNK, H, dv)
    b = jnp.cumsum(log_a, axis=2)
    tri = jnp.tril(jnp.ones((C_CHUNK, C_CHUNK), bool))
    diff = b[:, :, :, None] - b[:, :, None, :]
    decay = jnp.exp(jnp.where(tri[None, None, :, :, None, None], diff, -jnp.inf))
    att = jnp.einsum('bnthk,bnshk,bntshk->bnhts', q, k, decay)
    o_intra = jnp.einsum('bnhts,bnshv->bnthv', att, v)
    b_last = b[:, :, -1]
    du = jnp.einsum('bnshk,bnshv->bnhkv', k * jnp.exp(b_last[:, :, None] - b), v)

    def step(S, inp):
        a, u = inp
        return a[..., None] * S + u, S

    S0 = jnp.zeros((Bt, H, dk, dv), f32)
    _, S_prev = lax.scan(step, S0, (jnp.exp(b_last).transpose(1, 0, 2, 3), du.transpose(1, 0, 2, 3, 4)))
    o_inter = jnp.einsum('bnthk,nbhkv->bnthv', q * jnp.exp(b), S_prev)
    return (o_intra + o_inter).reshape(Bt, T, H, dv)


def mixer_c(q, k, v, g, zf, zb, wf, bf, wb, bb, norm_g):
    Bt, T = q.shape[:2]
    dt = q.dtype
    q = q.astype(f32).reshape(Bt, T, C_HEADS, C_DK) * C_DK ** -0.5
    k = k.astype(f32).reshape(Bt, T, C_HEADS, C_DK)
    v = v.astype(f32).reshape(Bt, T, C_HEADS, C_DV)

    def log_decay(z, w, bias):
        pre = jnp.einsum('btr,rk->btk', z.astype(f32), w.astype(f32)) + bias.astype(f32)
        return (jax.nn.log_sigmoid(pre) / C_GATE_NORM).reshape(Bt, T, C_HEADS, C_DK)

    flip = lambda a: a[:, ::-1]
    o_f = gla_direction(q, k, v, log_decay(zf, wf, bf))
    o_b = flip(gla_direction(flip(q), flip(k), flip(v), flip(log_decay(zb, wb, bb))))
    o = rms_norm(o_f + o_b, norm_g)
    gate = jax.nn.silu(g.astype(f32)).reshape(Bt, T, C_HEADS, C_DV)
    return (o * gate).reshape(Bt, T, C_V_W).astype(dt)


def expert_choice_ffn(h, w_router, w_gate, w_up, w_down):
    Bt, T, D = h.shape
    n = Bt * T
    cap = EC_CAPACITY * n // N_EXPERTS
    xf = h.reshape(n, D)
    aff = jax.nn.softmax(jnp.einsum('nd,de->ne', xf, w_router, preferred_element_type=f32), axis=-1)
    gate, idx = lax.top_k(aff.T, cap)
    xe = xf[idx]
    hid = jax.nn.silu(jnp.einsum('ecd,edf->ecf', xe, w_gate)) * jnp.einsum('ecd,edf->ecf', xe, w_up)
    ye = jnp.einsum('ecf,efd->ecd', hid, w_down) * gate[..., None].astype(h.dtype)
    y = jnp.zeros((n, D), h.dtype).at[idx.reshape(-1)].add(ye.reshape(-1, D))
    return y.reshape(Bt, T, D)


def trunk(x, ln_in_g, ln_in_b, rel_bias, w_in, q_norm_a, k_norm_a, gla_wf, gla_bf, gla_wb, gla_bb,
          gla_norm, w_out, ln1_g, ln1_b, w_router, w_gate, w_up, w_down, ln2_g, ln2_b):
    Bt, T, _ = x.shape
    x = layer_norm(x, ln_in_g, ln_in_b)
    for l in range(DEPTH):
        proj = jnp.einsum('btd,de->bte', x, w_in[l])
        aq, ak, av, bq, bk, bv, cq, ck, cv, cg, czf, czb = split_columns(proj)
        o_a = mixer_a(aq.reshape(Bt, T, A_HEADS, HEAD_DIM), ak.reshape(Bt, T, A_KV_HEADS, HEAD_DIM),
                      av.reshape(Bt, T, A_KV_HEADS, HEAD_DIM), q_norm_a[l], k_norm_a[l])
        o_b = mixer_b(bq.reshape(Bt, T, B_HEADS, HEAD_DIM), bk.reshape(Bt, T, B_HEADS, HEAD_DIM),
                      bv.reshape(Bt, T, B_HEADS, HEAD_DIM), rel_bias)
        o_c = mixer_c(cq, ck, cv, cg, czf, czb, gla_wf[l], gla_bf[l], gla_wb[l], gla_bb[l], gla_norm[l])
        mix = jnp.einsum('btm,md->btd', jnp.concatenate([o_a, o_b, o_c], -1), w_out[l])
        x = layer_norm(DN_ALPHA * x + mix, ln1_g[l], ln1_b[l])
        ffn = expert_choice_ffn(x, w_router[l], w_gate[l], w_up[l], w_down[l])
        x = layer_norm(DN_ALPHA * x + ffn, ln2_g[l], ln2_b[l])
    return x


def setup_inputs(seed: int = 0) -> dict:
    key = jax.random.key(seed)
    ks = jax.random.split(key, 24)
    nrm = lambda k, shape, scale: jax.random.normal(k, shape, f32) * scale
    gain = lambda k, shape: 1.0 + 0.02 * jax.random.normal(k, shape, f32)
    return {
        "x_prompt": nrm(ks[0], (BATCH, SEQ, D_MODEL), 1.0),
        "x_sample": nrm(ks[1], (DEC_BATCH, DEC_SEQ, D_MODEL), 1.0),
        "ln_in_g": gain(ks[2], (D_MODEL,)),
        "ln_in_b": nrm(ks[3], (D_MODEL,), 0.02),
        "rel_bias": nrm(ks[4], (REL_BUCKETS, B_HEADS), 0.5),
        "w_in": nrm(ks[5], (DEPTH, D_MODEL, D_IN), D_MODEL ** -0.5),
        "q_norm_a": gain(ks[6], (DEPTH, HEAD_DIM)),
        "k_norm_a": gain(ks[7], (DEPTH, HEAD_DIM)),
        "gla_wf": nrm(ks[8], (DEPTH, C_GATE_RANK, C_K_W), C_GATE_RANK ** -0.5),
        "gla_bf": nrm(ks[9], (DEPTH, C_K_W), 0.1),
        "gla_wb": nrm(ks[10], (DEPTH, C_GATE_RANK, C_K_W), C_GATE_RANK ** -0.5),
        "gla_bb": nrm(ks[11], (DEPTH, C_K_W), 0.1),
        "gla_norm": gain(ks[12], (DEPTH, C_DV)),
        "w_out": nrm(ks[13], (DEPTH, D_MIX, D_MODEL), DN_BETA * D_MIX ** -0.5),
        "ln1_g": gain(ks[14], (DEPTH, D_MODEL)),
        "ln1_b": nrm(ks[15], (DEPTH, D_MODEL), 0.02),
        "w_router": nrm(ks[16], (DEPTH, D_MODEL, N_EXPERTS), D_MODEL ** -0.5),
        "w_gate": nrm(ks[17], (DEPTH, N_EXPERTS, D_MODEL, D_FF_EXPERT), D_MODEL ** -0.5),
        "w_up": nrm(ks[18], (DEPTH, N_EXPERTS, D_MODEL, D_FF_EXPERT), D_MODEL ** -0.5),
        "w_down": nrm(ks[19], (DEPTH, N_EXPERTS, D_FF_EXPERT, D_MODEL), DN_BETA * D_FF_EXPERT ** -0.5),
        "ln2_g": gain(ks[20], (DEPTH, D_MODEL)),
        "ln2_b": nrm(ks[21], (DEPTH, D_MODEL), 0.02),
    }


def reference(x_prompt, x_sample, ln_in_g, ln_in_b, rel_bias, w_in, q_norm_a, k_norm_a, gla_wf, gla_bf,
              gla_wb, gla_bb, gla_norm, w_out, ln1_g, ln1_b, w_router, w_gate, w_up, w_down, ln2_g, ln2_b):
    y_prompt = trunk(x_prompt, ln_in_g, ln_in_b, rel_bias, w_in, q_norm_a, k_norm_a, gla_wf, gla_bf,
                     gla_wb, gla_bb, gla_norm, w_out, ln1_g, ln1_b, w_router, w_gate, w_up, w_down, ln2_g, ln2_b)
    y_sample = trunk(x_sample, ln_in_g, ln_in_b, rel_bias, w_in, q_norm_a, k_norm_a, gla_wf, gla_bf,
                     gla_wb, gla_bb, gla_norm, w_out, ln1_g, ln1_b, w_router, w_gate, w_up, w_down, ln2_g, ln2_b)
    return (y_prompt, y_sample)
```

```python
import functools
import math

import jax
import jax.numpy as jnp
from jax import lax
from jax.experimental import pallas as pl
from jax.experimental.pallas import tpu as pltpu

F32 = jnp.float32
BF16 = jnp.bfloat16
I32 = jnp.int32

D_MODEL = 2048
DEPTH = 2
HEAD_DIM = 128
A_HEADS = 6
A_KV_HEADS = 2
A_GROUP = A_HEADS // A_KV_HEADS
B_PAIRS = ((128, 1), (512, 4), (2048, 16))
B_R = 64
C_HEADS = 4
C_DK = 64
C_DV = 128
C_GATE_RANK = 16
C_GATE_NORM = 16.0
C_CHUNK = 16
GRID_W = 64
ROPE_THETA = 10000.0
REL_BUCKETS = 32
REL_MAX_DIST = 1024
N_EXPERTS = 16
EC_CAPACITY = 2
D_FF = 2048
DN_ALPHA = (2 * DEPTH) ** 0.25
LN_EPS = 1e-5
RMS_EPS = 1e-6
NEG_INF = -1e30

A_W = A_HEADS * HEAD_DIM
A_KV_W = A_KV_HEADS * HEAD_DIM
B_W = 6 * HEAD_DIM
C_K_W = C_HEADS * C_DK
C_V_W = C_HEADS * C_DV
D_MAIN = 5120
OFF_AQ, OFF_AK, OFF_AV = 0, 768, 1024
OFF_BQ, OFF_BK, OFF_BV = 1280, 2048, 2816
OFF_CQ, OFF_CK, OFF_CV, OFF_CG = 3584, 3840, 4096, 4608

V7X_VMEM_BYTES = 64 * 1024 * 1024
MAX_PREFETCH_SLOTS = 16384
HIGHEST = lax.Precision.HIGHEST
NT_DIMS = (((1,), (1,)), ((), ()))
TN_DIMS = (((0,), (0,)), ((), ()))


def _params(sem, vmem_mb):
    return pltpu.CompilerParams(dimension_semantics=sem, vmem_limit_bytes=vmem_mb * 1024 * 1024)


def _layer_norm(x, g, b):
    mu = jnp.mean(x, -1, keepdims=True)
    xc = x - mu
    var = jnp.mean(xc * xc, -1, keepdims=True)
    return xc * lax.rsqrt(var + LN_EPS) * g + b


def _sigmoid(x):
    return 1.0 / (1.0 + jnp.exp(-x))


def _ln_kernel(x_ref, g_ref, b_ref, of_ref, ob_ref):
    y = _layer_norm(x_ref[...], g_ref[...], b_ref[...])
    of_ref[...] = y
    ob_ref[...] = y.astype(BF16)


def layer_norm_rows(x, g, b):
    n = x.shape[0]
    tm = min(512, n)
    row = pl.BlockSpec((tm, D_MODEL), lambda i: (i, 0))
    vec = pl.BlockSpec((1, D_MODEL), lambda i: (0, 0))
    return pl.pallas_call(
        _ln_kernel,
        grid=(n // tm,),
        in_specs=[row, vec, vec],
        out_specs=[row, row],
        out_shape=[jax.ShapeDtypeStruct((n, D_MODEL), F32), jax.ShapeDtypeStruct((n, D_MODEL), BF16)],
        compiler_params=_params(("parallel",), 40),
    )(x, g.reshape(1, -1), b.reshape(1, -1))


def _mm_kernel(x_ref, w_ref, o_ref):
    o_ref[...] = jnp.dot(x_ref[...], w_ref[...], preferred_element_type=F32)


def matmul_f32out(x, w, tn):
    n, k = x.shape
    c = w.shape[1]
    tm = min(1024, n)
    return pl.pallas_call(
        _mm_kernel,
        grid=(n // tm, c // tn),
        in_specs=[pl.BlockSpec((tm, k), lambda i, j: (i, 0)), pl.BlockSpec((k, tn), lambda i, j: (0, j))],
        out_specs=pl.BlockSpec((tm, tn), lambda i, j: (i, j)),
        out_shape=jax.ShapeDtypeStruct((n, c), F32),
        compiler_params=_params(("parallel", "arbitrary"), 48),
    )(x, w)


def rope_tables(T):
    rows = T // GRID_W
    r = jnp.broadcast_to(jnp.arange(rows)[:, None], (rows, GRID_W)).reshape(T).astype(F32)
    c = jnp.broadcast_to(jnp.arange(GRID_W)[None, :], (rows, GRID_W)).reshape(T).astype(F32)
    axis_dim = HEAD_DIM // 2
    inv = ROPE_THETA ** (-jnp.arange(0, axis_dim, 2, dtype=F32) / axis_dim)
    ang_r = r[:, None] * inv[None, :]
    ang_c = c[:, None] * inv[None, :]
    cr, sr, cc, sc = jnp.cos(ang_r), jnp.sin(ang_r), jnp.cos(ang_c), jnp.sin(ang_c)
    cos = jnp.concatenate([cr, cr, cc, cc], -1)
    sin = jnp.concatenate([-sr, sr, -sc, sc], -1)
    return cos, sin


def _rope_kernel(q_ref, k_ref, v_ref, cos_ref, sin_ref, gq_ref, gk_ref, qo_ref, ko_ref, vo_ref):
    cos = cos_ref[...]
    sin = sin_ref[...]
    lane = lax.broadcasted_iota(I32, cos.shape, 1)
    first_half = (lane & 32) == 0

    def norm_rope(x, g):
        r = x * lax.rsqrt(jnp.mean(x * x, -1, keepdims=True) + RMS_EPS) * g
        partner = jnp.where(first_half, pltpu.roll(r, 96, 1), pltpu.roll(r, 32, 1))
        return r * cos + partner * sin

    gq = gq_ref[...]
    gk = gk_ref[...]
    for h in range(A_HEADS):
        sl = slice(h * HEAD_DIM, (h + 1) * HEAD_DIM)
        qo_ref[:, sl] = (norm_rope(q_ref[:, sl], gq) * HEAD_DIM ** -0.5).astype(BF16)
    for h in range(A_KV_HEADS):
        sl = slice(h * HEAD_DIM, (h + 1) * HEAD_DIM)
        ko_ref[:, sl] = norm_rope(k_ref[:, sl], gk).astype(BF16)
    vo_ref[...] = v_ref[...].astype(BF16)


def rope_qkv(proj, cos, sin, gq, gk, T):
    n = proj.shape[0]
    tm = min(512, T)
    tpb = T // tm
    tab = pl.BlockSpec((tm, HEAD_DIM), lambda i: (i % tpb, 0))
    vec = pl.BlockSpec((1, HEAD_DIM), lambda i: (0, 0))
    return pl.pallas_call(
        _rope_kernel,
        grid=(n // tm,),
        in_specs=[
            pl.BlockSpec((tm, A_W), lambda i: (i, OFF_AQ // A_W)),
            pl.BlockSpec((tm, A_KV_W), lambda i: (i, OFF_AK // A_KV_W)),
            pl.BlockSpec((tm, A_KV_W), lambda i: (i, OFF_AV // A_KV_W)),
            tab, tab, vec, vec,
        ],
        out_specs=[
            pl.BlockSpec((tm, A_W), lambda i: (i, 0)),
            pl.BlockSpec((tm, A_KV_W), lambda i: (i, 0)),
            pl.BlockSpec((tm, A_KV_W), lambda i: (i, 0)),
        ],
        out_shape=[
            jax.ShapeDtypeStruct((n, A_W), BF16),
            jax.ShapeDtypeStruct((n, A_KV_W), BF16),
            jax.ShapeDtypeStruct((n, A_KV_W), BF16),
        ],
        compiler_params=_params(("parallel",), 32),
    )(proj, proj, proj, cos, sin, gq.reshape(1, -1), gk.reshape(1, -1))


def _flash_kernel(q_ref, k_ref, v_ref, o_ref, m_ref, l_ref, acc_ref, *, nk):
    ki = pl.program_id(3)

    @pl.when(ki == 0)
    def _():
        m_ref[...] = jnp.full(m_ref.shape, NEG_INF, F32)
        l_ref[...] = jnp.zeros(l_ref.shape, F32)
        acc_ref[...] = jnp.zeros(acc_ref.shape, F32)

    k = k_ref[...]
    v = v_ref[...]
    for h in range(A_GROUP):
        q = q_ref[:, h * HEAD_DIM:(h + 1) * HEAD_DIM]
        s = lax.dot_general(q, k, NT_DIMS, preferred_element_type=F32)
        m_old = m_ref[h]
        m_new = jnp.maximum(m_old, jnp.max(s, -1, keepdims=True))
        p = jnp.exp(s - m_new)
        alpha = jnp.exp(m_old - m_new)
        l_ref[h] = alpha * l_ref[h] + jnp.sum(p, -1, keepdims=True)
        acc_ref[h] = alpha * acc_ref[h] + jnp.dot(p.astype(BF16), v, preferred_element_type=F32)
        m_ref[h] = m_new

    @pl.when(ki == nk - 1)
    def _():
        for h in range(A_GROUP):
            o_ref[:, h * HEAD_DIM:(h + 1) * HEAD_DIM] = (acc_ref[h] / l_ref[h]).astype(BF16)


def flash_attention(q, k, v, Bt, T):
    n = q.shape[0]
    tq = min(512, T)
    tk = min(1024, T)
    nq, nk = T // tq, T // tk
    gw = A_GROUP * HEAD_DIM
    return pl.pallas_call(
        functools.partial(_flash_kernel, nk=nk),
        grid=(Bt, A_KV_HEADS, nq, nk),
        in_specs=[
            pl.BlockSpec((tq, gw), lambda b, g, i, j: (b * nq + i, g)),
            pl.BlockSpec((tk, HEAD_DIM), lambda b, g, i, j: (b * nk + j, g)),
            pl.BlockSpec((tk, HEAD_DIM), lambda b, g, i, j: (b * nk + j, g)),
        ],
        out_specs=pl.BlockSpec((tq, gw), lambda b, g, i, j: (b * nq + i, g)),
        out_shape=jax.ShapeDtypeStruct((n, A_W), BF16),
        scratch_shapes=[
            pltpu.VMEM((A_GROUP, tq, 1), F32),
            pltpu.VMEM((A_GROUP, tq, 1), F32),
            pltpu.VMEM((A_GROUP, tq, HEAD_DIM), F32),
        ],
        compiler_params=_params(("parallel", "parallel", "parallel", "arbitrary"), 40),
    )(q, k, v)


def t5_bucket(rel):
    half = REL_BUCKETS // 2
    max_exact = half // 2
    n = jnp.abs(rel)
    large = max_exact + (jnp.log(jnp.maximum(n, 1).astype(F32) / max_exact)
                         / math.log(REL_MAX_DIST / max_exact) * (half - max_exact)).astype(I32)
    large = jnp.minimum(large, half - 1)
    return jnp.where(rel > 0, half, 0) + jnp.where(n < max_exact, n, large)


def dilated_bias(rel_bias, g, dil, Q):
    rel = jnp.arange(-B_R, B_R + 1)
    vals = rel_bias.astype(F32)[t5_bucket(rel * dil)][:, 2 * g:2 * g + 2]
    off = jnp.arange(Q + 2 * B_R)[None, :] - jnp.arange(Q)[:, None]
    inside = (off >= 0) & (off <= 2 * B_R)
    band = jnp.where(inside[..., None], vals[jnp.clip(off, 0, 2 * B_R)], 0.0)
    return band.transpose(2, 0, 1)


def _dilated_kernel(q_ref, kp_ref, kc_ref, kn_ref, vp_ref, vc_ref, vn_ref, bias_ref, o_ref, lse_ref,
                    *, dil, Q, nt):
    j = pl.program_id(1)
    W = Q + 2 * B_R
    qi = lax.broadcasted_iota(I32, (Q, W), 0)
    ci = lax.broadcasted_iota(I32, (Q, W), 1)
    band = jnp.abs(ci - B_R - qi) <= B_R
    valid = band & ((ci >= B_R) | (j > 0)) & ((ci < Q + B_R) | (j < nt - 1))

    def rows(ref, r, size):
        if dil == 1:
            return ref[pl.ds(0, size), :]
        return ref[pl.ds(r, size, stride=dil), :]

    def residue(r, carry):
        q = rows(q_ref, r, Q).astype(BF16)
        k = jnp.concatenate([rows(kp_ref, r, B_R), rows(kc_ref, r, Q), rows(kn_ref, r, B_R)], 0).astype(BF16)
        v = jnp.concatenate([rows(vp_ref, r, B_R), rows(vc_ref, r, Q), rows(vn_ref, r, B_R)], 0).astype(BF16)
        s = lax.dot_general(q, k, NT_DIMS, preferred_element_type=F32)
        s = s * HEAD_DIM ** -0.5 + bias_ref[...]
        s = jnp.where(valid, s, NEG_INF)
        m = jnp.max(s, -1, keepdims=True)
        p = jnp.exp(s - m)
        den = jnp.sum(p, -1, keepdims=True)
        o = jnp.dot((p / den).astype(BF16), v, preferred_element_type=F32)
        lse = jnp.broadcast_to(m + jnp.log(den), (Q, HEAD_DIM))
        if dil == 1:
            o_ref[...] = o
            lse_ref[...] = lse
        else:
            o_ref[pl.ds(r, Q, stride=dil), :] = o
            lse_ref[pl.ds(r, Q, stride=dil), :] = lse
        return carry

    if dil == 1:
        residue(0, 0)
    else:
        lax.fori_loop(0, dil, residue, 0)


def dilated_group(proj, bias, g, dil, Bt, T):
    n = proj.shape[0]
    Q = 128
    tb = Q * dil
    hb = B_R * dil
    nt = T // tb
    per = tb // hb
    nhb = T // hb
    hd = HEAD_DIM
    qcol, kcol, vcol = (OFF_BQ // hd + 2 * g, OFF_BK // hd + 2 * g, OFF_BV // hd + 2 * g)

    def cur(col):
        return pl.BlockSpec((tb, hd), lambda b, j, h: (b * nt + j, col + h))

    def prev(col):
        return pl.BlockSpec((hb, hd), lambda b, j, h: (b * nhb + jnp.maximum(j * per - 1, 0), col + h))

    def nxt(col):
        return pl.BlockSpec((hb, hd), lambda b, j, h: (b * nhb + jnp.minimum((j + 1) * per, nhb - 1), col + h))

    out = pl.BlockSpec((tb, hd), lambda b, j, h: (b * nt + j, h))
    return pl.pallas_call(
        functools.partial(_dilated_kernel, dil=dil, Q=Q, nt=nt),
        grid=(Bt, nt, 2),
        in_specs=[cur(qcol), prev(kcol), cur(kcol), nxt(kcol), prev(vcol), cur(vcol), nxt(vcol),
                  pl.BlockSpec((None, Q, Q + 2 * B_R), lambda b, j, h: (h, 0, 0))],
        out_specs=[out, out],
        out_shape=[jax.ShapeDtypeStruct((n, 2 * hd), F32), jax.ShapeDtypeStruct((n, 2 * hd), F32)],
        compiler_params=_params(("parallel", "parallel", "parallel"), 48),
    )(proj, proj, proj, proj, proj, proj, proj, bias)


def _dilated_combine_kernel(o0_ref, o1_ref, o2_ref, l0_ref, l1_ref, l2_ref, out_ref):
    o_refs = (o0_ref, o1_ref, o2_ref)
    l_refs = (l0_ref, l1_ref, l2_ref)
    for h in range(2):
        sl = slice(h * HEAD_DIM, (h + 1) * HEAD_DIM)
        ls = [r[:, sl] for r in l_refs]
        m = jnp.maximum(jnp.maximum(ls[0], ls[1]), ls[2])
        es = [jnp.exp(l - m) for l in ls]
        den = es[0] + es[1] + es[2]
        for g in range(3):
            col = g * 2 * HEAD_DIM + h * HEAD_DIM
            out_ref[:, col:col + HEAD_DIM] = (o_refs[g][:, sl] * (es[g] / den)).astype(BF16)


def dilated_combine(outs, lses):
    n = outs[0].shape[0]
    tm = min(512, n)
    blk = pl.BlockSpec((tm, 2 * HEAD_DIM), lambda i: (i, 0))
    return pl.pallas_call(
        _dilated_combine_kernel,
        grid=(n // tm,),
        in_specs=[blk] * 6,
        out_specs=pl.BlockSpec((tm, B_W), lambda i: (i, 0)),
        out_shape=jax.ShapeDtypeStruct((n, B_W), BF16),
        compiler_params=_params(("parallel",), 32),
    )(*outs, *lses)


def _gla_kernel(q_ref, k_ref, v_ref, z_ref, w_ref, b_ref, hs_ref, o_ref, st_ref, la_ref, *, reverse, tb):
    i = pl.program_id(1)

    @pl.when(i == 0)
    def _():
        st_ref[...] = jnp.zeros(st_ref.shape, F32)

    pre = jnp.dot(z_ref[...], w_ref[...], precision=HIGHEST, preferred_element_type=F32) + b_ref[...]
    la_ref[...] = (jnp.minimum(pre, 0.0) - jnp.log1p(jnp.exp(-jnp.abs(pre)))) / C_GATE_NORM

    ti = lax.broadcasted_iota(I32, (C_CHUNK, C_CHUNK), 0)
    si = lax.broadcasted_iota(I32, (C_CHUNK, C_CHUNK), 1)
    tri = jnp.where((ti <= si) if reverse else (ti >= si), 1.0, 0.0).astype(F32)
    trow = lax.broadcasted_iota(I32, (C_CHUNK, 1), 0)
    last = 0 if reverse else C_CHUNK - 1
    nchunk = tb // C_CHUNK
    hs = hs_ref[...]

    def chunk(ci, carry):
        c = (nchunk - 1 - ci) if reverse else ci
        r0 = pl.multiple_of(c * C_CHUNK, C_CHUNK)
        la = la_ref[pl.ds(r0, C_CHUNK), :]
        b = jnp.dot(tri, la, precision=HIGHEST, preferred_element_type=F32)
        btot = b[last:last + 1]
        q = q_ref[pl.ds(r0, C_CHUNK), :] * C_DK ** -0.5
        k = k_ref[pl.ds(r0, C_CHUNK), :]
        v = v_ref[pl.ds(r0, C_CHUNK), :]
        rows = []
        for s in range(C_CHUNK):
            seen = (trow <= s) if reverse else (trow >= s)
            decay = jnp.exp(jnp.where(seen, b - b[s:s + 1], NEG_INF))
            rows.append(q * decay * k[s:s + 1])
        w_all = jnp.concatenate(rows, 0).astype(BF16)
        att = jnp.dot(w_all, hs, preferred_element_type=F32)
        o = jnp.zeros((C_CHUNK, C_V_W), F32)
        for s in range(C_CHUNK):
            o = o + att[s * C_CHUNK:(s + 1) * C_CHUNK] * v[s:s + 1]
        qd = q * jnp.exp(b)
        kd = k * jnp.exp(btot - b)
        atot = jnp.exp(btot)
        parts = []
        for h in range(C_HEADS):
            ks = slice(h * C_DK, (h + 1) * C_DK)
            vs = slice(h * C_DV, (h + 1) * C_DV)
            st = st_ref[h]
            parts.append(lax.dot_general(qd[:, ks], st, NT_DIMS, preferred_element_type=F32))
            du = lax.dot_general(v[:, vs], kd[:, ks], TN_DIMS, preferred_element_type=F32)
            st_ref[h] = st * atot[:, ks] + du
        o_ref[pl.ds(r0, C_CHUNK), :] = o + jnp.concatenate(parts, 1)
        return carry

    lax.fori_loop(0, nchunk, chunk, 0)


def gla_direction(proj, z, w, bias, Bt, T, reverse):
    n = proj.shape[0]
    tb = min(256, T)
    nb = T // tb
    hs = (jnp.arange(C_K_W)[:, None] // C_DK == jnp.arange(C_V_W)[None, :] // C_DV).astype(BF16)

    def blk(width, col):
        if reverse:
            return pl.BlockSpec((tb, width), lambda b, i: (b * nb + nb - 1 - i, col))
        return pl.BlockSpec((tb, width), lambda b, i: (b * nb + i, col))

    return pl.pallas_call(
        functools.partial(_gla_kernel, reverse=reverse, tb=tb),
        grid=(Bt, nb),
        in_specs=[
            blk(C_K_W, OFF_CQ // C_K_W), blk(C_K_W, OFF_CK // C_K_W), blk(C_V_W, OFF_CV // C_V_W),
            blk(HEAD_DIM, 0),
            pl.BlockSpec((HEAD_DIM, C_K_W), lambda b, i: (0, 0)),
            pl.BlockSpec((1, C_K_W), lambda b, i: (0, 0)),
            pl.BlockSpec((C_K_W, C_V_W), lambda b, i: (0, 0)),
        ],
        out_specs=blk(C_V_W, 0),
        out_shape=jax.ShapeDtypeStruct((n, C_V_W), F32),
        scratch_shapes=[pltpu.VMEM((C_HEADS, C_DV, C_DK), F32), pltpu.VMEM((tb, C_K_W), F32)],
        compiler_params=_params(("parallel", "arbitrary"), 32),
    )(proj, proj, proj, z, w, bias.reshape(1, -1), hs)


def _gla_out_kernel(of_ref, ob_ref, g_ref, ng_ref, out_ref):
    ng = ng_ref[...]
    for h in range(C_HEADS):
        sl = slice(h * C_DV, (h + 1) * C_DV)
        o = of_ref[:, sl] + ob_ref[:, sl]
        o = o * lax.rsqrt(jnp.mean(o * o, -1, keepdims=True) + RMS_EPS) * ng
        g = g_ref[:, sl]
        out_ref[:, sl] = (o * (g * _sigmoid(g))).astype(BF16)


def gla_output(o_f, o_b, proj, norm_g):
    n = o_f.shape[0]
    tm = min(512, n)
    blk = pl.BlockSpec((tm, C_V_W), lambda i: (i, 0))
    return pl.pallas_call(
        _gla_out_kernel,
        grid=(n // tm,),
        in_specs=[blk, blk, pl.BlockSpec((tm, C_V_W), lambda i: (i, OFF_CG // C_V_W)),
                  pl.BlockSpec((1, C_DV), lambda i: (0, 0))],
        out_specs=blk,
        out_shape=jax.ShapeDtypeStruct((n, C_V_W), BF16),
        compiler_params=_params(("parallel",), 32),
    )(o_f, o_b, proj, norm_g.reshape(1, -1))


def _outproj_kernel(oa_ref, ob_ref, oc_ref, x_ref, w_ref, g_ref, b_ref, x1_ref, y0_ref):
    mix = jnp.dot(oa_ref[...], w_ref[0:A_W, :], preferred_element_type=F32)
    mix = mix + jnp.dot(ob_ref[...], w_ref[A_W:A_W + B_W, :], preferred_element_type=F32)
    mix = mix + jnp.dot(oc_ref[...], w_ref[A_W + B_W:, :], preferred_element_type=F32)
    x1 = _layer_norm(DN_ALPHA * x_ref[...] + mix, g_ref[...], b_ref[...])
    x1_ref[...] = x1
    y0_ref[...] = DN_ALPHA * x1


def outproj_ln(oa, ob, oc, x, w, g, b):
    n = x.shape[0]
    tm = min(256, n)
    row = pl.BlockSpec((tm, D_MODEL), lambda i: (i, 0))
    vec = pl.BlockSpec((1, D_MODEL), lambda i: (0, 0))
    return pl.pallas_call(
        _outproj_kernel,
        grid=(n // tm,),
        in_specs=[pl.BlockSpec((tm, A_W), lambda i: (i, 0)), pl.BlockSpec((tm, B_W), lambda i: (i, 0)),
                  pl.BlockSpec((tm, C_V_W), lambda i: (i, 0)), row,
                  pl.BlockSpec((D_MODEL, D_MODEL), lambda i: (0, 0)), vec, vec],
        out_specs=[row, row],
        out_shape=[jax.ShapeDtypeStruct((n, D_MODEL), F32), jax.ShapeDtypeStruct((n, D_MODEL), F32)],
        compiler_params=_params(("parallel",), 48),
    )(oa, ob, oc, x, w, g.reshape(1, -1), b.reshape(1, -1))


def _router_kernel(x_ref, wt_ref, aff_ref):
    lg = lax.dot_general(wt_ref[...], x_ref[...], NT_DIMS, precision=HIGHEST, preferred_element_type=F32)
    e = jnp.exp(lg - jnp.max(lg, 0, keepdims=True))
    aff_ref[...] = e / jnp.sum(e, 0, keepdims=True)


def router_affinity(x, w_router_t):
    n = x.shape[0]
    tm = min(512, n)
    return pl.pallas_call(
        _router_kernel,
        grid=(n // tm,),
        in_specs=[pl.BlockSpec((tm, D_MODEL), lambda i: (i, 0)),
                  pl.BlockSpec((N_EXPERTS, D_MODEL), lambda i: (0, 0))],
        out_specs=pl.BlockSpec((N_EXPERTS, tm), lambda i: (0, i)),
        out_shape=jax.ShapeDtypeStruct((N_EXPERTS, n), F32),
        compiler_params=_params(("parallel",), 32),
    )(x, w_router_t)


def _select_kernel(aff_ref, rank_ref, *, cap, n, ch):
    keys = pltpu.bitcast(aff_ref[...], I32)

    def bit_step(i, thr):
        cand = thr | jnp.left_shift(jnp.int32(1), 30 - i)
        cnt = jnp.sum((keys >= cand).astype(I32), axis=1, keepdims=True)
        return jnp.where(cnt >= cap, cand, thr)

    thr = lax.fori_loop(0, 31, bit_step, jnp.zeros((N_EXPERTS, 1), I32))
    need = (cap - jnp.sum((keys > thr).astype(I32), axis=1, keepdims=True)).astype(F32)
    upper = (lax.broadcasted_iota(I32, (ch, ch), 0) <= lax.broadcasted_iota(I32, (ch, ch), 1))
    upper = jnp.where(upper, 1.0, 0.0).astype(BF16)

    def chunk(c, carry):
        seen_eq, seen_sel = carry
        sl = pl.ds(pl.multiple_of(c * ch, ch), ch)
        kc = pltpu.bitcast(aff_ref[:, sl], I32)
        eq = kc == thr
        cum_eq = jnp.dot(jnp.where(eq, 1.0, 0.0).astype(BF16), upper, preferred_element_type=F32) + seen_eq
        sel = (kc > thr) | (eq & (cum_eq <= need))
        cum_sel = jnp.dot(jnp.where(sel, 1.0, 0.0).astype(BF16), upper, preferred_element_type=F32) + seen_sel
        rank_ref[:, sl] = jnp.where(sel, cum_sel - 1.0, -1.0).astype(I32)
        return cum_eq[:, ch - 1:ch], cum_sel[:, ch - 1:ch]

    zero = jnp.zeros((N_EXPERTS, 1), F32)
    lax.fori_loop(0, n // ch, chunk, (zero, zero))


def select_top_capacity(aff, cap):
    n = aff.shape[1]
    ch = min(512, n)
    return pl.pallas_call(
        functools.partial(_select_kernel, cap=cap, n=n, ch=ch),
        out_shape=jax.ShapeDtypeStruct((N_EXPERTS, n), I32),
        compiler_params=pltpu.CompilerParams(vmem_limit_bytes=32 * 1024 * 1024),
    )(aff)


def _gather_kernel(idx_ref, x_hbm, o_ref, sem, *, tm):
    base = pl.program_id(0) * tm

    def issue(i, carry):
        t = idx_ref[base + i]
        pltpu.make_async_copy(x_hbm.at[pl.ds(t, 1), :], o_ref.at[pl.ds(i, 1), :], sem).start()
        return carry

    lax.fori_loop(0, tm, issue, 0)
    pltpu.make_async_copy(x_hbm.at[pl.ds(0, tm), :], o_ref, sem).wait()


def gather_rows(idx, x, tm):
    slots = idx.shape[0]
    return pl.pallas_call(
        functools.partial(_gather_kernel, tm=tm),
        grid_spec=pltpu.PrefetchScalarGridSpec(
            num_scalar_prefetch=1,
            grid=(slots // tm,),
            in_specs=[pl.BlockSpec(memory_space=pl.ANY)],
            out_specs=pl.BlockSpec((tm, D_MODEL), lambda i, idx_ref: (i, 0)),
            scratch_shapes=[pltpu.SemaphoreType.DMA],
        ),
        out_shape=jax.ShapeDtypeStruct((slots, D_MODEL), F32),
        compiler_params=_params(("arbitrary",), 32),
    )(idx, x)


def _ffn_kernel(xe_ref, wg_ref, wu_ref, wd_ref, gate_ref, o_ref, xb_ref, acc_ref, *, nf):
    f = pl.program_id(2)

    @pl.when(f == 0)
    def _():
        xb_ref[...] = xe_ref[...].astype(BF16)
        acc_ref[...] = jnp.zeros(acc_ref.shape, F32)

    xb = xb_ref[...]
    g = jnp.dot(xb, wg_ref[...], preferred_element_type=F32)
    u = jnp.dot(xb, wu_ref[...], preferred_element_type=F32)
    hid = (g * _sigmoid(g)) * u
    acc_ref[...] += jnp.dot(hid.astype(BF16), wd_ref[...], preferred_element_type=F32)

    @pl.when(f == nf - 1)
    def _():
        o_ref[...] = acc_ref[...] * gate_ref[...]


def expert_ffn(xe, gate, wg, wu, wd, e0, ne, cap, tm):
    slots = xe.shape[0]
    tf = 512
    nf = D_FF // tf
    tpe = cap // tm
    return pl.pallas_call(
        functools.partial(_ffn_kernel, nf=nf),
        grid=(ne, tpe, nf),
        in_specs=[
            pl.BlockSpec((tm, D_MODEL), lambda e, i, f: (e * tpe + i, 0)),
            pl.BlockSpec((None, D_MODEL, tf), lambda e, i, f: (e0 + e, 0, f)),
            pl.BlockSpec((None, D_MODEL, tf), lambda e, i, f: (e0 + e, 0, f)),
            pl.BlockSpec((None, tf, D_MODEL), lambda e, i, f: (e0 + e, f, 0)),
            pl.BlockSpec((tm, 1), lambda e, i, f: (e * tpe + i, 0)),
        ],
        out_specs=pl.BlockSpec((tm, D_MODEL), lambda e, i, f: (e * tpe + i, 0)),
        out_shape=jax.ShapeDtypeStruct((slots, D_MODEL), F32),
        scratch_shapes=[pltpu.VMEM((tm, D_MODEL), BF16), pltpu.VMEM((tm, D_MODEL), F32)],
        compiler_params=_params(("parallel", "parallel", "arbitrary"), 48),
    )(xe, wg, wu, wd, gate)


def _scatter_kernel(idx_ref, ye_ref, y_in_hbm, y_hbm, buf, sem, *, tm):
    del y_in_hbm
    base = pl.program_id(0) * tm

    def fetch(i, carry):
        t = idx_ref[base + i]
        pltpu.make_async_copy(y_hbm.at[pl.ds(t, 1), :], buf.at[pl.ds(i, 1), :], sem.at[0]).start()
        return carry

    lax.fori_loop(0, tm, fetch, 0)
    pltpu.make_async_copy(y_hbm.at[pl.ds(0, tm), :], buf, sem.at[0]).wait()
    buf[...] = buf[...] + ye_ref[...]

    def put(i, carry):
        t = idx_ref[base + i]
        pltpu.make_async_copy(buf.at[pl.ds(i, 1), :], y_hbm.at[pl.ds(t, 1), :], sem.at[1]).start()
        return carry

    lax.fori_loop(0, tm, put, 0)
    pltpu.make_async_copy(buf, y_hbm.at[pl.ds(0, tm), :], sem.at[1]).wait()


def scatter_add_rows(idx, ye, y, tm):
    slots = idx.shape[0]
    return pl.pallas_call(
        functools.partial(_scatter_kernel, tm=tm),
        grid_spec=pltpu.PrefetchScalarGridSpec(
            num_scalar_prefetch=1,
            grid=(slots // tm,),
            in_specs=[pl.BlockSpec((tm, D_MODEL), lambda i, idx_ref: (i, 0)),
                      pl.BlockSpec(memory_space=pl.ANY)],
            out_specs=pl.BlockSpec(memory_space=pl.ANY),
            scratch_shapes=[pltpu.VMEM((tm, D_MODEL), F32), pltpu.SemaphoreType.DMA((2,))],
        ),
        out_shape=jax.ShapeDtypeStruct(y.shape, F32),
        input_output_aliases={2: 0},
        compiler_params=_params(("arbitrary",), 32),
    )(idx, ye, y)


def expert_choice_ffn(x1, y0, w_router_t, wg, wu, wd):
    n = x1.shape[0]
    cap = EC_CAPACITY * n // N_EXPERTS
    aff = router_affinity(x1, w_router_t)
    rank = select_top_capacity(aff, cap)
    tok = jnp.broadcast_to(jnp.arange(n, dtype=I32)[None, :], rank.shape)
    keys = jnp.where(rank >= 0, tok, n + tok)
    _, idx, gate = lax.sort((keys, tok, aff), dimension=1, num_keys=1)
    idx = idx[:, :cap]
    gate = gate[:, :cap]
    tm = min(512, cap)
    ne = max(1, min(N_EXPERTS, MAX_PREFETCH_SLOTS // cap))
    y = y0
    for e0 in range(0, N_EXPERTS, ne):
        idx_g = idx[e0:e0 + ne].reshape(-1)
        gate_g = gate[e0:e0 + ne].reshape(-1, 1)
        xe = gather_rows(idx_g, x1, tm)
        ye = expert_ffn(xe, gate_g, wg, wu, wd, e0, ne, cap, tm)
        y = scatter_add_rows(idx_g, ye, y, tm)
    return y


def _trunk(x, Bt, T, p):
    n = Bt * T
    xf, xb = layer_norm_rows(x.reshape(n, D_MODEL), p["ln_in_g"], p["ln_in_b"])
    cos, sin = rope_tables(T)
    for l in range(DEPTH):
        proj = matmul_f32out(xb, p["w_in_main"][l], 1024)
        z = matmul_f32out(xb, p["w_in_tail"][l], HEAD_DIM)
        qa, ka, va = rope_qkv(proj, cos, sin, p["q_norm_a"][l], p["k_norm_a"][l], T)
        o_a = flash_attention(qa, ka, va, Bt, T)
        outs, lses = [], []
        for g, (_, dil) in enumerate(B_PAIRS):
            o, lse = dilated_group(proj, p["b_bias"][g], g, dil, Bt, T)
            outs.append(o)
            lses.append(lse)
        o_b = dilated_combine(outs, lses)
        c_f = gla_direction(proj, z, p["gla_wf"][l], p["gla_bf"][l], Bt, T, False)
        c_b = gla_direction(proj, z, p["gla_wb"][l], p["gla_bb"][l], Bt, T, True)
        o_c = gla_output(c_f, c_b, proj, p["gla_norm"][l])
        x1, y0 = outproj_ln(o_a, o_b, o_c, xf, p["w_out"][l], p["ln1_g"][l], p["ln1_b"][l])
        y = expert_choice_ffn(x1, y0, p["w_router_t"][l], p["w_gate"][l], p["w_up"][l], p["w_down"][l])
        xf, xb = layer_norm_rows(y, p["ln2_g"][l], p["ln2_b"][l])
    return xf.reshape(Bt, T, D_MODEL)


def _prepare(ln_in_g, ln_in_b, rel_bias, w_in, q_norm_a, k_norm_a, gla_wf, gla_bf, gla_wb, gla_bb,
             gla_norm, w_out, ln1_g, ln1_b, w_router, w_gate, w_up, w_down, ln2_g, ln2_b):
    tail = jnp.pad(w_in[:, :, D_MAIN:], ((0, 0), (0, 0), (0, HEAD_DIM - 2 * C_GATE_RANK)))
    pad_f = ((0, 0), (0, HEAD_DIM - C_GATE_RANK), (0, 0))
    pad_b = ((0, 0), (C_GATE_RANK, HEAD_DIM - 2 * C_GATE_RANK), (0, 0))
    return dict(
        ln_in_g=ln_in_g, ln_in_b=ln_in_b,
        w_in_main=w_in[:, :, :D_MAIN].astype(BF16), w_in_tail=tail.astype(BF16),
        q_norm_a=q_norm_a, k_norm_a=k_norm_a,
        gla_wf=jnp.pad(gla_wf, pad_f), gla_bf=gla_bf, gla_wb=jnp.pad(gla_wb, pad_b), gla_bb=gla_bb,
        gla_norm=gla_norm, w_out=w_out.astype(BF16), ln1_g=ln1_g, ln1_b=ln1_b,
        w_router_t=jnp.swapaxes(w_router, 1, 2),
        w_gate=w_gate.astype(BF16), w_up=w_up.astype(BF16), w_down=w_down.astype(BF16),
        ln2_g=ln2_g, ln2_b=ln2_b,
        b_bias=[dilated_bias(rel_bias, g, dil, 128) for g, (_, dil) in enumerate(B_PAIRS)],
    )


def kernel(x_prompt, x_sample, ln_in_g, ln_in_b, rel_bias, w_in, q_norm_a, k_norm_a, gla_wf, gla_bf, gla_wb,
           gla_bb, gla_norm, w_out, ln1_g, ln1_b, w_router, w_gate, w_up, w_down, ln2_g, ln2_b):
    p = _prepare(ln_in_g, ln_in_b, rel_bias, w_in, q_norm_a, k_norm_a, gla_wf, gla_bf, gla_wb, gla_bb,
                 gla_norm, w_out, ln1_g, ln1_b, w_router, w_gate, w_up, w_down, ln2_g, ln2_b)
    y_prompt = _trunk(x_prompt, x_prompt.shape[0], x_prompt.shape[1], p)
    y_sample = _trunk(x_sample, x_sample.shape[0], x_sample.shape[1], p)
    return (y_prompt, y_sample)
```

```python
import functools
import math

import jax
import jax.numpy as jnp
from jax import lax
from jax.experimental import pallas as pl
from jax.experimental.pallas import tpu as pltpu

F32 = jnp.float32
BF16 = jnp.bfloat16
I32 = jnp.int32

D_MODEL = 2048
DEPTH = 2
HEAD_DIM = 128
A_HEADS = 6
A_KV_HEADS = 2
A_GROUP = A_HEADS // A_KV_HEADS
B_PAIRS = ((128, 1), (512, 4), (2048, 16))
B_R = 64
C_HEADS = 4
C_DK = 64
C_DV = 128
C_GATE_RANK = 16
C_GATE_NORM = 16.0
C_CHUNK = 16
GRID_W = 64
ROPE_THETA = 10000.0
REL_BUCKETS = 32
REL_MAX_DIST = 1024
N_EXPERTS = 16
EC_CAPACITY = 2
D_FF = 2048
DN_ALPHA = (2 * DEPTH) ** 0.25
LN_EPS = 1e-5
RMS_EPS = 1e-6
NEG_INF = -1e30

A_W = A_HEADS * HEAD_DIM
A_KV_W = A_KV_HEADS * HEAD_DIM
B_W = 6 * HEAD_DIM
C_K_W = C_HEADS * C_DK
C_V_W = C_HEADS * C_DV
D_MAIN = 5120
OFF_AQ, OFF_AK, OFF_AV = 0, 768, 1024
OFF_BQ, OFF_BK, OFF_BV = 1280, 2048, 2816
OFF_CQ, OFF_CK, OFF_CV, OFF_CG = 3584, 3840, 4096, 4608

V7X_VMEM_BYTES = 64 * 1024 * 1024
MAX_PREFETCH_SLOTS = 16384
DMA_ISSUE_UNROLL = 8
HIGHEST = lax.Precision.HIGHEST
NT_DIMS = (((1,), (1,)), ((), ()))
TN_DIMS = (((0,), (0,)), ((), ()))


def _params(sem, vmem_mb):
    return pltpu.CompilerParams(dimension_semantics=sem, vmem_limit_bytes=vmem_mb * 1024 * 1024)


def _layer_norm(x, g, b):
    mu = jnp.mean(x, -1, keepdims=True)
    xc = x - mu
    var = jnp.mean(xc * xc, -1, keepdims=True)
    return xc * lax.rsqrt(var + LN_EPS) * g + b


def _sigmoid(x):
    return 1.0 / (1.0 + jnp.exp(-x))


def _ln_kernel(x_ref, g_ref, b_ref, of_ref, ob_ref):
    y = _layer_norm(x_ref[...], g_ref[...], b_ref[...])
    of_ref[...] = y
    ob_ref[...] = y.astype(BF16)


def layer_norm_rows(x, g, b):
    n = x.shape[0]
    tm = min(512, n)
    row = pl.BlockSpec((tm, D_MODEL), lambda i: (i, 0))
    vec = pl.BlockSpec((1, D_MODEL), lambda i: (0, 0))
    return pl.pallas_call(
        _ln_kernel,
        grid=(n // tm,),
        in_specs=[row, vec, vec],
        out_specs=[row, row],
        out_shape=[jax.ShapeDtypeStruct((n, D_MODEL), F32), jax.ShapeDtypeStruct((n, D_MODEL), BF16)],
        compiler_params=_params(("parallel",), 40),
        name="layer_norm",
    )(x, g.reshape(1, -1), b.reshape(1, -1))


def _mm_kernel(x_ref, w_ref, o_ref, wb_ref):
    @pl.when(pl.program_id(1) == 0)
    def _():
        wb_ref[...] = w_ref[...].astype(BF16)

    o_ref[...] = jnp.dot(x_ref[...], wb_ref[...], preferred_element_type=F32)


def input_projection(x, w_in, layer, ncols, tn):
    n, k = x.shape
    tm = min(1024, n)
    return pl.pallas_call(
        _mm_kernel,
        grid=(ncols // tn, n // tm),
        in_specs=[pl.BlockSpec((tm, k), lambda j, i: (i, 0)),
                  pl.BlockSpec((None, k, tn), lambda j, i: (layer, 0, j))],
        out_specs=pl.BlockSpec((tm, tn), lambda j, i: (i, j)),
        out_shape=jax.ShapeDtypeStruct((n, ncols), F32),
        scratch_shapes=[pltpu.VMEM((k, tn), BF16)],
        compiler_params=_params(("parallel", "arbitrary"), 48),
        name="input_projection",
    )(x, w_in)


def rope_tables(T):
    rows = T // GRID_W
    r = jnp.broadcast_to(jnp.arange(rows)[:, None], (rows, GRID_W)).reshape(T).astype(F32)
    c = jnp.broadcast_to(jnp.arange(GRID_W)[None, :], (rows, GRID_W)).reshape(T).astype(F32)
    axis_dim = HEAD_DIM // 2
    inv = ROPE_THETA ** (-jnp.arange(0, axis_dim, 2, dtype=F32) / axis_dim)
    ang_r = r[:, None] * inv[None, :]
    ang_c = c[:, None] * inv[None, :]
    cr, sr, cc, sc = jnp.cos(ang_r), jnp.sin(ang_r), jnp.cos(ang_c), jnp.sin(ang_c)
    cos = jnp.concatenate([cr, cr, cc, cc], -1)
    sin = jnp.concatenate([-sr, sr, -sc, sc], -1)
    return cos, sin


def _rope_kernel(q_ref, k_ref, v_ref, cos_ref, sin_ref, gq_ref, gk_ref, qo_ref, ko_ref, vo_ref):
    cos = cos_ref[...]
    sin = sin_ref[...]
    lane = lax.broadcasted_iota(I32, cos.shape, 1)
    first_half = (lane & 32) == 0

    def norm_rope(x, g):
        r = x * lax.rsqrt(jnp.mean(x * x, -1, keepdims=True) + RMS_EPS) * g
        partner = jnp.where(first_half, pltpu.roll(r, 96, 1), pltpu.roll(r, 32, 1))
        return r * cos + partner * sin

    gq = gq_ref[...]
    gk = gk_ref[...]
    for h in range(A_HEADS):
        sl = slice(h * HEAD_DIM, (h + 1) * HEAD_DIM)
        qo_ref[:, sl] = (norm_rope(q_ref[:, sl], gq) * HEAD_DIM ** -0.5).astype(BF16)
    for h in range(A_KV_HEADS):
        sl = slice(h * HEAD_DIM, (h + 1) * HEAD_DIM)
        ko_ref[:, sl] = norm_rope(k_ref[:, sl], gk).astype(BF16)
    vo_ref[...] = v_ref[...].astype(BF16)


def rope_qkv(proj, cos, sin, gq, gk, T):
    n = proj.shape[0]
    tm = min(512, T)
    tpb = T // tm
    tab = pl.BlockSpec((tm, HEAD_DIM), lambda i: (i % tpb, 0))
    vec = pl.BlockSpec((1, HEAD_DIM), lambda i: (0, 0))
    return pl.pallas_call(
        _rope_kernel,
        grid=(n // tm,),
        in_specs=[
            pl.BlockSpec((tm, A_W), lambda i: (i, OFF_AQ // A_W)),
            pl.BlockSpec((tm, A_KV_W), lambda i: (i, OFF_AK // A_KV_W)),
            pl.BlockSpec((tm, A_KV_W), lambda i: (i, OFF_AV // A_KV_W)),
            tab, tab, vec, vec,
        ],
        out_specs=[
            pl.BlockSpec((tm, A_W), lambda i: (i, 0)),
            pl.BlockSpec((tm, A_KV_W), lambda i: (i, 0)),
            pl.BlockSpec((tm, A_KV_W), lambda i: (i, 0)),
        ],
        out_shape=[
            jax.ShapeDtypeStruct((n, A_W), BF16),
            jax.ShapeDtypeStruct((n, A_KV_W), BF16),
            jax.ShapeDtypeStruct((n, A_KV_W), BF16),
        ],
        compiler_params=_params(("parallel",), 32),
        name="rope_qkv",
    )(proj, proj, proj, cos, sin, gq.reshape(1, -1), gk.reshape(1, -1))


def _flash_kernel(q_ref, k_ref, v_ref, o_ref, m_ref, l_ref, acc_ref, *, nk):
    ki = pl.program_id(3)

    @pl.when(ki == 0)
    def _():
        m_ref[...] = jnp.full(m_ref.shape, NEG_INF, F32)
        l_ref[...] = jnp.zeros(l_ref.shape, F32)
        acc_ref[...] = jnp.zeros(acc_ref.shape, F32)

    k = k_ref[...]
    v = v_ref[...]
    for h in range(A_GROUP):
        q = q_ref[:, h * HEAD_DIM:(h + 1) * HEAD_DIM]
        s = lax.dot_general(q, k, NT_DIMS, preferred_element_type=F32)
        m_old = m_ref[h]
        m_new = jnp.maximum(m_old, jnp.max(s, -1, keepdims=True))
        p = jnp.exp(s - m_new)
        alpha = jnp.exp(m_old - m_new)
        l_ref[h] = alpha * l_ref[h] + jnp.sum(p, -1, keepdims=True)
        acc_ref[h] = alpha * acc_ref[h] + jnp.dot(p.astype(BF16), v, preferred_element_type=F32)
        m_ref[h] = m_new

    @pl.when(ki == nk - 1)
    def _():
        for h in range(A_GROUP):
            o_ref[:, h * HEAD_DIM:(h + 1) * HEAD_DIM] = (acc_ref[h] / l_ref[h]).astype(BF16)


def flash_attention(q, k, v, Bt, T):
    n = q.shape[0]
    tq = min(512, T)
    tk = min(1024, T)
    nq, nk = T // tq, T // tk
    gw = A_GROUP * HEAD_DIM
    return pl.pallas_call(
        functools.partial(_flash_kernel, nk=nk),
        grid=(Bt, A_KV_HEADS, nq, nk),
        in_specs=[
            pl.BlockSpec((tq, gw), lambda b, g, i, j: (b * nq + i, g)),
            pl.BlockSpec((tk, HEAD_DIM), lambda b, g, i, j: (b * nk + j, g)),
            pl.BlockSpec((tk, HEAD_DIM), lambda b, g, i, j: (b * nk + j, g)),
        ],
        out_specs=pl.BlockSpec((tq, gw), lambda b, g, i, j: (b * nq + i, g)),
        out_shape=jax.ShapeDtypeStruct((n, A_W), BF16),
        scratch_shapes=[
            pltpu.VMEM((A_GROUP, tq, 1), F32),
            pltpu.VMEM((A_GROUP, tq, 1), F32),
            pltpu.VMEM((A_GROUP, tq, HEAD_DIM), F32),
        ],
        compiler_params=_params(("parallel", "parallel", "parallel", "arbitrary"), 40),
        name="flash_attention",
    )(q, k, v)


def t5_bucket(rel):
    half = REL_BUCKETS // 2
    max_exact = half // 2
    n = jnp.abs(rel)
    large = max_exact + (jnp.log(jnp.maximum(n, 1).astype(F32) / max_exact)
                         / math.log(REL_MAX_DIST / max_exact) * (half - max_exact)).astype(I32)
    large = jnp.minimum(large, half - 1)
    return jnp.where(rel > 0, half, 0) + jnp.where(n < max_exact, n, large)


def dilated_bias(rel_bias, g, dil, Q):
    rel = jnp.arange(-B_R, B_R + 1)
    table = rel_bias.astype(F32)[:, 2 * g:2 * g + 2]
    pick = t5_bucket(rel * dil)[:, None] == jnp.arange(REL_BUCKETS)[None, :]
    vals = jnp.sum(jnp.where(pick[:, :, None], table[None], 0.0), axis=1)
    off = jnp.arange(Q + 2 * B_R)[None, :] - jnp.arange(Q)[:, None]
    hit = off[:, :, None] == jnp.arange(2 * B_R + 1)[None, None, :]
    band = jnp.sum(jnp.where(hit[..., None], vals[None, None], 0.0), axis=2)
    return band.transpose(2, 0, 1)


B_Q = 128
B_TILE_TOKENS = 2048


def _dilated_kernel(*refs, dil, nt, units):
    q_refs, kp_refs, kc_refs, kn_refs = refs[0:2], refs[2:4], refs[4:6], refs[6:8]
    vp_refs, vc_refs, vn_refs = refs[8:10], refs[10:12], refs[12:14]
    bias_ref = refs[14]
    o_refs, lse_refs = refs[15:17], refs[17:19]
    j = pl.program_id(1)
    W = B_Q + 2 * B_R
    qi = lax.broadcasted_iota(I32, (B_Q, W), 0)
    ci = lax.broadcasted_iota(I32, (B_Q, W), 1)
    band = jnp.abs(ci - B_R - qi) <= B_R
    head_ok = (ci >= B_R) | (j > 0)
    tail_ok = (ci < B_Q + B_R) | (j < nt - 1)

    def rows(ref, r, start, size):
        if dil == 1:
            return ref[start:start + size, :]
        return ref[pl.ds(r + start * dil, size, stride=dil), :]

    def window(prev_ref, cur_ref, next_ref, r, u):
        lo = u * B_Q - B_R
        parts = []
        if u == 0:
            parts.append(rows(prev_ref, r, 0, B_R))
        c0 = max(lo, 0)
        c1 = min(lo + W, units * B_Q)
        parts.append(rows(cur_ref, r, c0, c1 - c0))
        if u == units - 1:
            parts.append(rows(next_ref, r, 0, B_R))
        return (parts[0] if len(parts) == 1 else jnp.concatenate(parts, 0)).astype(BF16)

    for h in range(2):
        bias = bias_ref[h]
        for r in range(dil):
            for u in range(units):
                valid = band
                if u == 0:
                    valid = valid & head_ok
                if u == units - 1:
                    valid = valid & tail_ok
                q = rows(q_refs[h], r, u * B_Q, B_Q).astype(BF16)
                k = window(kp_refs[h], kc_refs[h], kn_refs[h], r, u)
                v = window(vp_refs[h], vc_refs[h], vn_refs[h], r, u)
                s = lax.dot_general(q, k, NT_DIMS, preferred_element_type=F32)
                s = jnp.where(valid, s * HEAD_DIM ** -0.5 + bias, NEG_INF)
                m = jnp.max(s, -1, keepdims=True)
                p = jnp.exp(s - m)
                den = jnp.sum(p, -1, keepdims=True)
                o = jnp.dot((p / den).astype(BF16), v, preferred_element_type=F32)
                lse = jnp.broadcast_to(m + jnp.log(den), (B_Q, HEAD_DIM))
                if dil == 1:
                    o_refs[h][u * B_Q:(u + 1) * B_Q, :] = o
                    lse_refs[h][u * B_Q:(u + 1) * B_Q, :] = lse
                else:
                    o_refs[h][pl.ds(r + u * B_Q * dil, B_Q, stride=dil), :] = o
                    lse_refs[h][pl.ds(r + u * B_Q * dil, B_Q, stride=dil), :] = lse


def dilated_group(proj, bias, g, dil, Bt, T):
    n = proj.shape[0]
    tb = min(B_TILE_TOKENS, T)
    units = tb // (B_Q * dil)
    hb = B_R * dil
    nt = T // tb
    per = tb // hb
    nhb = T // hb
    hd = HEAD_DIM
    qcol, kcol, vcol = (OFF_BQ // hd + 2 * g, OFF_BK // hd + 2 * g, OFF_BV // hd + 2 * g)

    def cur(col):
        return [pl.BlockSpec((tb, hd), lambda b, j, h=h: (b * nt + j, col + h)) for h in range(2)]

    def prev(col):
        return [pl.BlockSpec((hb, hd), lambda b, j, h=h: (b * nhb + jnp.maximum(j * per - 1, 0), col + h))
                for h in range(2)]

    def nxt(col):
        return [pl.BlockSpec((hb, hd), lambda b, j, h=h: (b * nhb + jnp.minimum((j + 1) * per, nhb - 1), col + h))
                for h in range(2)]

    out = pl.BlockSpec((tb, hd), lambda b, j: (b * nt + j, 0))
    in_specs = (cur(qcol) + prev(kcol) + cur(kcol) + nxt(kcol) + prev(vcol) + cur(vcol) + nxt(vcol)
                + [pl.BlockSpec((2, B_Q, B_Q + 2 * B_R), lambda b, j: (0, 0, 0))])
    return pl.pallas_call(
        functools.partial(_dilated_kernel, dil=dil, nt=nt, units=units),
        grid=(Bt, nt),
        in_specs=in_specs,
        out_specs=[out] * 4,
        out_shape=[jax.ShapeDtypeStruct((n, hd), F32)] * 4,
        compiler_params=_params(("parallel", "parallel"), 48),
        name=f"dilated_d{dil}",
    )(*([proj] * 14), bias)


def _dilated_combine_kernel(*refs):
    out_ref = refs[12]
    for h in range(2):
        ls = [refs[4 * g + 2 + h][...] for g in range(3)]
        m = jnp.maximum(jnp.maximum(ls[0], ls[1]), ls[2])
        es = [jnp.exp(l - m) for l in ls]
        den = es[0] + es[1] + es[2]
        for g in range(3):
            col = g * 2 * HEAD_DIM + h * HEAD_DIM
            out_ref[:, col:col + HEAD_DIM] = (refs[4 * g + h][...] * (es[g] / den)).astype(BF16)


def dilated_combine(groups):
    n = groups[0][0].shape[0]
    tm = min(512, n)
    blk = pl.BlockSpec((tm, HEAD_DIM), lambda i: (i, 0))
    return pl.pallas_call(
        _dilated_combine_kernel,
        grid=(n // tm,),
        in_specs=[blk] * 12,
        out_specs=pl.BlockSpec((tm, B_W), lambda i: (i, 0)),
        out_shape=jax.ShapeDtypeStruct((n, B_W), BF16),
        compiler_params=_params(("parallel",), 32),
        name="dilated_combine",
    )(*[a for grp in groups for a in grp])


def _gla_kernel(q_ref, k_ref, v_ref, z_ref, w_ref, b_ref, hs_ref, o_ref, st_ref, *, reverse, tb):
    i = pl.program_id(1)

    @pl.when(i == 0)
    def _():
        st_ref[...] = jnp.zeros(st_ref.shape, F32)

    nc = tb // C_CHUNK
    pre = jnp.dot(z_ref[...], w_ref[...], precision=HIGHEST, preferred_element_type=F32) + b_ref[...]
    la = (jnp.minimum(pre, 0.0) - jnp.log1p(jnp.exp(-jnp.abs(pre)))) / C_GATE_NORM

    ti = lax.broadcasted_iota(I32, (tb, tb), 0)
    si = lax.broadcasted_iota(I32, (tb, tb), 1)
    same = (ti // C_CHUNK) == (si // C_CHUNK)
    tri = jnp.where(same & ((ti <= si) if reverse else (ti >= si)), 1.0, 0.0).astype(F32)
    b = jnp.dot(tri, la, precision=HIGHEST, preferred_element_type=F32)

    last = 0 if reverse else C_CHUNK - 1
    b3 = b.reshape(nc, C_CHUNK, C_K_W)
    q3 = (q_ref[...] * C_DK ** -0.5).reshape(nc, C_CHUNK, C_K_W)
    k3 = k_ref[...].reshape(nc, C_CHUNK, C_K_W)
    v3 = v_ref[...].reshape(nc, C_CHUNK, C_V_W)
    trow = lax.broadcasted_iota(I32, (1, C_CHUNK, 1), 1)
    hs = hs_ref[...]

    o = jnp.zeros((tb, C_V_W), F32)
    for s in range(C_CHUNK):
        seen = (trow <= s) if reverse else (trow >= s)
        decay = jnp.exp(jnp.where(seen, b3 - b3[:, s:s + 1, :], NEG_INF))
        w_s = (q3 * decay * k3[:, s:s + 1, :]).reshape(tb, C_K_W).astype(BF16)
        att = jnp.dot(w_s, hs, preferred_element_type=F32)
        v_s = jnp.broadcast_to(v3[:, s:s + 1, :], (nc, C_CHUNK, C_V_W)).reshape(tb, C_V_W)
        o = o + att * v_s

    btot3 = b3[:, last:last + 1, :]
    qd3 = q3 * jnp.exp(b3)
    kd3 = k3 * jnp.exp(btot3 - b3)
    atot3 = jnp.exp(btot3)
    st = [st_ref[h] for h in range(C_HEADS)]
    for ci in range(nc):
        c = (nc - 1 - ci) if reverse else ci
        parts = []
        for h in range(C_HEADS):
            ks = slice(h * C_DK, (h + 1) * C_DK)
            vs = slice(h * C_DV, (h + 1) * C_DV)
            parts.append(lax.dot_general(qd3[c][:, ks], st[h], NT_DIMS, preferred_element_type=F32))
            du = lax.dot_general(v3[c][:, vs], kd3[c][:, ks], TN_DIMS, preferred_element_type=F32)
            st[h] = st[h] * atot3[c][:, ks] + du
        rows = slice(c * C_CHUNK, (c + 1) * C_CHUNK)
        o_ref[rows, :] = o[rows] + jnp.concatenate(parts, 1)
    for h in range(C_HEADS):
        st_ref[h] = st[h]


def gla_direction(proj, z, w, bias, Bt, T, reverse):
    n = proj.shape[0]
    tb = min(256, T)
    nb = T // tb
    hs = (jnp.arange(C_K_W)[:, None] // C_DK == jnp.arange(C_V_W)[None, :] // C_DV).astype(BF16)

    def blk(width, col):
        if reverse:
            return pl.BlockSpec((tb, width), lambda b, i: (b * nb + nb - 1 - i, col))
        return pl.BlockSpec((tb, width), lambda b, i: (b * nb + i, col))

    return pl.pallas_call(
        functools.partial(_gla_kernel, reverse=reverse, tb=tb),
        grid=(Bt, nb),
        in_specs=[
            blk(C_K_W, OFF_CQ // C_K_W), blk(C_K_W, OFF_CK // C_K_W), blk(C_V_W, OFF_CV // C_V_W),
            blk(HEAD_DIM, 0),
            pl.BlockSpec((HEAD_DIM, C_K_W), lambda b, i: (0, 0)),
            pl.BlockSpec((1, C_K_W), lambda b, i: (0, 0)),
            pl.BlockSpec((C_K_W, C_V_W), lambda b, i: (0, 0)),
        ],
        out_specs=blk(C_V_W, 0),
        out_shape=jax.ShapeDtypeStruct((n, C_V_W), F32),
        scratch_shapes=[pltpu.VMEM((C_HEADS, C_DV, C_DK), F32)],
        compiler_params=_params(("parallel", "arbitrary"), 32),
        name="gla_bwd" if reverse else "gla_fwd",
    )(proj, proj, proj, z, w, bias.reshape(1, -1), hs)


def _gla_out_kernel(of_ref, ob_ref, g_ref, ng_ref, out_ref):
    ng = ng_ref[...]
    for h in range(C_HEADS):
        sl = slice(h * C_DV, (h + 1) * C_DV)
        o = of_ref[:, sl] + ob_ref[:, sl]
        o = o * lax.rsqrt(jnp.mean(o * o, -1, keepdims=True) + RMS_EPS) * ng
        g = g_ref[:, sl]
        out_ref[:, sl] = (o * (g * _sigmoid(g))).astype(BF16)


def gla_output(o_f, o_b, proj, norm_g):
    n = o_f.shape[0]
    tm = min(512, n)
    blk = pl.BlockSpec((tm, C_V_W), lambda i: (i, 0))
    return pl.pallas_call(
        _gla_out_kernel,
        grid=(n // tm,),
        in_specs=[blk, blk, pl.BlockSpec((tm, C_V_W), lambda i: (i, OFF_CG // C_V_W)),
                  pl.BlockSpec((1, C_DV), lambda i: (0, 0))],
        out_specs=blk,
        out_shape=jax.ShapeDtypeStruct((n, C_V_W), BF16),
        compiler_params=_params(("parallel",), 32),
        name="gla_output",
    )(o_f, o_b, proj, norm_g.reshape(1, -1))


def _outproj_kernel(oa_ref, ob_ref, oc_ref, x_ref, w_ref, g_ref, b_ref, x1_ref, y0_ref):
    mix = jnp.dot(oa_ref[...], w_ref[0:A_W, :], preferred_element_type=F32)
    mix = mix + jnp.dot(ob_ref[...], w_ref[A_W:A_W + B_W, :], preferred_element_type=F32)
    mix = mix + jnp.dot(oc_ref[...], w_ref[A_W + B_W:, :], preferred_element_type=F32)
    x1 = _layer_norm(DN_ALPHA * x_ref[...] + mix, g_ref[...], b_ref[...])
    x1_ref[...] = x1
    y0_ref[...] = DN_ALPHA * x1


def outproj_ln(oa, ob, oc, x, w, g, b):
    n = x.shape[0]
    tm = min(256, n)
    row = pl.BlockSpec((tm, D_MODEL), lambda i: (i, 0))
    vec = pl.BlockSpec((1, D_MODEL), lambda i: (0, 0))
    return pl.pallas_call(
        _outproj_kernel,
        grid=(n // tm,),
        in_specs=[pl.BlockSpec((tm, A_W), lambda i: (i, 0)), pl.BlockSpec((tm, B_W), lambda i: (i, 0)),
                  pl.BlockSpec((tm, C_V_W), lambda i: (i, 0)), row,
                  pl.BlockSpec((D_MODEL, D_MODEL), lambda i: (0, 0)), vec, vec],
        out_specs=[row, row],
        out_shape=[jax.ShapeDtypeStruct((n, D_MODEL), F32), jax.ShapeDtypeStruct((n, D_MODEL), F32)],
        compiler_params=_params(("parallel",), 48),
        name="outproj_ln",
    )(oa, ob, oc, x, w, g.reshape(1, -1), b.reshape(1, -1))


def _router_kernel(x_ref, wt_ref, aff_ref):
    lg = lax.dot_general(wt_ref[...], x_ref[...], NT_DIMS, precision=HIGHEST, preferred_element_type=F32)
    e = jnp.exp(lg - jnp.max(lg, 0, keepdims=True))
    aff_ref[...] = e / jnp.sum(e, 0, keepdims=True)


def router_affinity(x, w_router_t):
    n = x.shape[0]
    tm = min(512, n)
    return pl.pallas_call(
        _router_kernel,
        grid=(n // tm,),
        in_specs=[pl.BlockSpec((tm, D_MODEL), lambda i: (i, 0)),
                  pl.BlockSpec((N_EXPERTS, D_MODEL), lambda i: (0, 0))],
        out_specs=pl.BlockSpec((N_EXPERTS, tm), lambda i: (0, i)),
        out_shape=jax.ShapeDtypeStruct((N_EXPERTS, n), F32),
        compiler_params=_params(("parallel",), 32),
        name="router",
    )(x, w_router_t)


def _select_kernel(aff_ref, rank_ref, *, cap, n, ch):
    keys = pltpu.bitcast(aff_ref[...], I32)

    def bit_step(i, thr):
        cand = thr | jnp.left_shift(jnp.int32(1), 30 - i)
        cnt = jnp.sum((keys >= cand).astype(I32), axis=1, keepdims=True)
        return jnp.where(cnt >= cap, cand, thr)

    thr = lax.fori_loop(0, 31, bit_step, jnp.zeros((N_EXPERTS, 1), I32))
    need = (cap - jnp.sum((keys > thr).astype(I32), axis=1, keepdims=True)).astype(F32)
    upper = (lax.broadcasted_iota(I32, (ch, ch), 0) <= lax.broadcasted_iota(I32, (ch, ch), 1))
    upper = jnp.where(upper, 1.0, 0.0).astype(BF16)

    def chunk(c, carry):
        seen_eq, seen_sel = carry
        sl = pl.ds(pl.multiple_of(c * ch, ch), ch)
        kc = pltpu.bitcast(aff_ref[:, sl], I32)
        eq = kc == thr
        cum_eq = jnp.dot(jnp.where(eq, 1.0, 0.0).astype(BF16), upper, preferred_element_type=F32) + seen_eq
        sel = (kc > thr) | (eq & (cum_eq <= need))
        cum_sel = jnp.dot(jnp.where(sel, 1.0, 0.0).astype(BF16), upper, preferred_element_type=F32) + seen_sel
        rank_ref[:, sl] = jnp.where(sel, cum_sel - 1.0, -1.0).astype(I32)
        return cum_eq[:, ch - 1:ch], cum_sel[:, ch - 1:ch]

    zero = jnp.zeros((N_EXPERTS, 1), F32)
    lax.fori_loop(0, n // ch, chunk, (zero, zero))


def select_top_capacity(aff, cap):
    n = aff.shape[1]
    ch = min(512, n)
    return pl.pallas_call(
        functools.partial(_select_kernel, cap=cap, n=n, ch=ch),
        out_shape=jax.ShapeDtypeStruct((N_EXPERTS, n), I32),
        compiler_params=pltpu.CompilerParams(vmem_limit_bytes=32 * 1024 * 1024),
        name="select_top_capacity",
    )(aff)


def _gather_kernel(idx_ref, x_hbm, o_ref, sem, *, tm):
    base = pl.program_id(0) * tm

    def issue(i, carry):
        t = idx_ref[base + i]
        pltpu.make_async_copy(x_hbm.at[pl.ds(t, 1), :], o_ref.at[pl.ds(i, 1), :], sem).start()
        return carry

    lax.fori_loop(0, tm, issue, 0, unroll=DMA_ISSUE_UNROLL)
    pltpu.make_async_copy(x_hbm.at[pl.ds(0, tm), :], o_ref, sem).wait()


def gather_rows(idx, x, tm):
    slots = idx.shape[0]
    return pl.pallas_call(
        functools.partial(_gather_kernel, tm=tm),
        grid_spec=pltpu.PrefetchScalarGridSpec(
            num_scalar_prefetch=1,
            grid=(slots // tm,),
            in_specs=[pl.BlockSpec(memory_space=pl.ANY)],
            out_specs=pl.BlockSpec((tm, D_MODEL), lambda i, idx_ref: (i, 0)),
            scratch_shapes=[pltpu.SemaphoreType.DMA],
        ),
        out_shape=jax.ShapeDtypeStruct((slots, D_MODEL), F32),
        compiler_params=_params(("arbitrary",), 32),
        name="gather_rows",
    )(idx, x)


def _ffn_kernel(xe_ref, wg_ref, wu_ref, wd_ref, gate_ref, o_ref, xb_ref, acc_ref, *, nf):
    f = pl.program_id(2)

    @pl.when(f == 0)
    def _():
        xb_ref[...] = xe_ref[...].astype(BF16)
        acc_ref[...] = jnp.zeros(acc_ref.shape, F32)

    xb = xb_ref[...]
    g = jnp.dot(xb, wg_ref[...], preferred_element_type=F32)
    u = jnp.dot(xb, wu_ref[...], preferred_element_type=F32)
    hid = (g * _sigmoid(g)) * u
    acc_ref[...] += jnp.dot(hid.astype(BF16), wd_ref[...], preferred_element_type=F32)

    @pl.when(f == nf - 1)
    def _():
        o_ref[...] = acc_ref[...] * gate_ref[...]


def expert_ffn(xe, gate, wg, wu, wd, e0, ne, cap, tm):
    slots = xe.shape[0]
    tf = 512
    nf = D_FF // tf
    tpe = cap // tm
    return pl.pallas_call(
        functools.partial(_ffn_kernel, nf=nf),
        grid=(ne, tpe, nf),
        in_specs=[
            pl.BlockSpec((tm, D_MODEL), lambda e, i, f: (e * tpe + i, 0)),
            pl.BlockSpec((None, D_MODEL, tf), lambda e, i, f: (e0 + e, 0, f)),
            pl.BlockSpec((None, D_MODEL, tf), lambda e, i, f: (e0 + e, 0, f)),
            pl.BlockSpec((None, tf, D_MODEL), lambda e, i, f: (e0 + e, f, 0)),
            pl.BlockSpec((tm, 1), lambda e, i, f: (e * tpe + i, 0)),
        ],
        out_specs=pl.BlockSpec((tm, D_MODEL), lambda e, i, f: (e * tpe + i, 0)),
        out_shape=jax.ShapeDtypeStruct((slots, D_MODEL), F32),
        scratch_shapes=[pltpu.VMEM((tm, D_MODEL), BF16), pltpu.VMEM((tm, D_MODEL), F32)],
        compiler_params=_params(("parallel", "parallel", "arbitrary"), 48),
        name="expert_ffn",
    )(xe, wg, wu, wd, gate)


def _scatter_kernel(idx_ref, ye_ref, y_in_hbm, y_hbm, buf, sem, *, tm):
    del y_in_hbm
    base = pl.program_id(0) * tm

    def fetch(i, carry):
        t = idx_ref[base + i]
        pltpu.make_async_copy(y_hbm.at[pl.ds(t, 1), :], buf.at[pl.ds(i, 1), :], sem.at[0]).start()
        return carry

    lax.fori_loop(0, tm, fetch, 0, unroll=DMA_ISSUE_UNROLL)
    pltpu.make_async_copy(y_hbm.at[pl.ds(0, tm), :], buf, sem.at[0]).wait()
    buf[...] = buf[...] + ye_ref[...]

    def put(i, carry):
        t = idx_ref[base + i]
        pltpu.make_async_copy(buf.at[pl.ds(i, 1), :], y_hbm.at[pl.ds(t, 1), :], sem.at[1]).start()
        return carry

    lax.fori_loop(0, tm, put, 0, unroll=DMA_ISSUE_UNROLL)
    pltpu.make_async_copy(buf, y_hbm.at[pl.ds(0, tm), :], sem.at[1]).wait()


def scatter_add_rows(idx, ye, y, tm):
    slots = idx.shape[0]
    return pl.pallas_call(
        functools.partial(_scatter_kernel, tm=tm),
        grid_spec=pltpu.PrefetchScalarGridSpec(
            num_scalar_prefetch=1,
            grid=(slots // tm,),
            in_specs=[pl.BlockSpec((tm, D_MODEL), lambda i, idx_ref: (i, 0)),
                      pl.BlockSpec(memory_space=pl.ANY)],
            out_specs=pl.BlockSpec(memory_space=pl.ANY),
            scratch_shapes=[pltpu.VMEM((tm, D_MODEL), F32), pltpu.SemaphoreType.DMA((2,))],
        ),
        out_shape=jax.ShapeDtypeStruct(y.shape, F32),
        input_output_aliases={2: 0},
        compiler_params=_params(("arbitrary",), 32),
        name="scatter_add_rows",
    )(idx, ye, y)


def expert_choice_ffn(x1, y0, w_router_t, wg, wu, wd):
    n = x1.shape[0]
    cap = EC_CAPACITY * n // N_EXPERTS
    aff = router_affinity(x1, w_router_t)
    rank = select_top_capacity(aff, cap)
    tok = jnp.arange(n, dtype=I32)[None, :]
    keys = jnp.where(rank >= 0, tok, n + tok)
    keys, gate = lax.sort((keys, aff), dimension=1, num_keys=1)
    idx = keys[:, :cap]
    gate = gate[:, :cap]
    tm = min(512, cap)
    ne = max(1, min(N_EXPERTS, MAX_PREFETCH_SLOTS // cap))
    y = y0
    for e0 in range(0, N_EXPERTS, ne):
        idx_g = idx[e0:e0 + ne].reshape(-1)
        gate_g = gate[e0:e0 + ne].reshape(-1, 1)
        xe = gather_rows(idx_g, x1, tm)
        ye = expert_ffn(xe, gate_g, wg, wu, wd, e0, ne, cap, tm)
        y = scatter_add_rows(idx_g, ye, y, tm)
    return y


def _trunk(x, Bt, T, p):
    n = Bt * T
    xf, xb = layer_norm_rows(x.reshape(n, D_MODEL), p["ln_in_g"], p["ln_in_b"])
    cos, sin = rope_tables(T)
    for l in range(DEPTH):
        proj = input_projection(xb, p["w_in"], l, D_MAIN, 1024)
        z = input_projection(xb, p["w_in_tail"], l, HEAD_DIM, HEAD_DIM)
        qa, ka, va = rope_qkv(proj, cos, sin, p["q_norm_a"][l], p["k_norm_a"][l], T)
        o_a = flash_attention(qa, ka, va, Bt, T)
        o_b = dilated_combine([dilated_group(proj, p["b_bias"][g], g, dil, Bt, T)
                               for g, (_, dil) in enumerate(B_PAIRS)])
        c_f = gla_direction(proj, z, p["gla_wf"][l], p["gla_bf"][l], Bt, T, False)
        c_b = gla_direction(proj, z, p["gla_wb"][l], p["gla_bb"][l], Bt, T, True)
        o_c = gla_output(c_f, c_b, proj, p["gla_norm"][l])
        x1, y0 = outproj_ln(o_a, o_b, o_c, xf, p["w_out"][l], p["ln1_g"][l], p["ln1_b"][l])
        y = expert_choice_ffn(x1, y0, p["w_router_t"][l], p["w_gate"][l], p["w_up"][l], p["w_down"][l])
        xf, xb = layer_norm_rows(y, p["ln2_g"][l], p["ln2_b"][l])
    return xf.reshape(Bt, T, D_MODEL)


def _prepare(ln_in_g, ln_in_b, rel_bias, w_in, q_norm_a, k_norm_a, gla_wf, gla_bf, gla_wb, gla_bb,
             gla_norm, w_out, ln1_g, ln1_b, w_router, w_gate, w_up, w_down, ln2_g, ln2_b):
    tail = jnp.pad(w_in[:, :, D_MAIN:], ((0, 0), (0, 0), (0, HEAD_DIM - 2 * C_GATE_RANK)))
    pad_f = ((0, 0), (0, HEAD_DIM - C_GATE_RANK), (0, 0))
    pad_b = ((0, 0), (C_GATE_RANK, HEAD_DIM - 2 * C_GATE_RANK), (0, 0))
    return dict(
        ln_in_g=ln_in_g, ln_in_b=ln_in_b,
        w_in=w_in, w_in_tail=tail,
        q_norm_a=q_norm_a, k_norm_a=k_norm_a,
        gla_wf=jnp.pad(gla_wf, pad_f), gla_bf=gla_bf, gla_wb=jnp.pad(gla_wb, pad_b), gla_bb=gla_bb,
        gla_norm=gla_norm, w_out=w_out.astype(BF16), ln1_g=ln1_g, ln1_b=ln1_b,
        w_router_t=jnp.swapaxes(w_router, 1, 2),
        w_gate=w_gate.astype(BF16), w_up=w_up.astype(BF16), w_down=w_down.astype(BF16),
        ln2_g=ln2_g, ln2_b=ln2_b,
        b_bias=[dilated_bias(rel_bias, g, dil, 128) for g, (_, dil) in enumerate(B_PAIRS)],
    )


def kernel(x_prompt, x_sample, ln_in_g, ln_in_b, rel_bias, w_in, q_norm_a, k_norm_a, gla_wf, gla_bf, gla_wb,
           gla_bb, gla_norm, w_out, ln1_g, ln1_b, w_router, w_gate, w_up, w_down, ln2_g, ln2_b):
    p = _prepare(ln_in_g, ln_in_b, rel_bias, w_in, q_norm_a, k_norm_a, gla_wf, gla_bf, gla_wb, gla_bb,
                 gla_norm, w_out, ln1_g, ln1_b, w_router, w_gate, w_up, w_down, ln2_g, ln2_b)
    y_prompt = _trunk(x_prompt, x_prompt.shape[0], x_prompt.shape[1], p)
    y_sample = _trunk(x_sample, x_sample.shape[0], x_sample.shape[1], p)
    return (y_prompt, y_sample)
```
